```python
import math
import jax, jax.numpy as jnp
from jax import lax
import numpy as np

D_MODEL = 1024
BATCH = 16
SEQ = 256
DEPTH = 1
DEC_BATCH = 2
DEC_SEQ = 2048
PAST_LEN = 256

GRID_W = 64
D_SSM = D_MODEL // 2
SSM_CH = 16
N_SSM_GROUPS = D_SSM // SSM_CH
SSM_STATE = 64
D_CONV = D_MODEL // 2
CONV_HEADS = 8
CONV_WIDTH = 3
D_MIX = D_SSM + D_CONV
D_IN = D_SSM + 3 * D_CONV
N_MOD = 6
PEER_HEADS = 8
PEER_NKEYS = 128
PEER_TOPK = 16
PEER_DQ = D_MODEL // 4
N_EXPERTS = PEER_NKEYS ** 2
TOKEN_BLOCK = 128
EPS = 1e-6

kernel_name = "hybrid_s5_shortconv_peer_diffusion_step"


def rmsnorm(x, g):
    xf = x.astype(jnp.float32)
    y = xf * lax.rsqrt(jnp.mean(xf * xf, axis=-1, keepdims=True) + EPS)
    return (y * g.astype(jnp.float32)).astype(x.dtype)


def _combine(left, right):
    a1, b1 = left
    a2, b2 = right
    return a1 * a2, a2 * b1 + b2


def s5_scan(u, h0, lam_re, lam_im, log_dt, b_re, b_im, c_re, c_im, reverse):
    f32 = jnp.float32
    lam = lax.complex(lam_re.astype(f32), lam_im.astype(f32))
    dt = jnp.exp(log_dt.astype(f32))[:, None]
    a_bar = jnp.exp(lam * dt)
    b_mat = lax.complex(b_re.astype(f32), b_im.astype(f32))
    b_bar = ((a_bar - 1.0) / lam)[..., None] * b_mat
    c_mat = lax.complex(c_re.astype(f32), c_im.astype(f32))
    bu = jnp.einsum("gnp,blgp->blgn", b_bar, u.astype(f32).astype(jnp.complex64))
    a = jnp.broadcast_to(a_bar, bu.shape)
    a_cum, h = lax.associative_scan(_combine, (a, bu), reverse=reverse, axis=1)
    h = h + a_cum * h0[:, None]
    y = jnp.einsum("gpn,blgn->blgp", c_mat, h).real
    h_last = h[:, 0] if reverse else h[:, -1]
    return y, h_last


def ssm_mixer(u, h0_f, h0_b, p):
    bn, seq_len, _ = u.shape
    ug = u.reshape(bn, seq_len, N_SSM_GROUPS, SSM_CH)
    y_f, h_f = s5_scan(ug, h0_f, p["lam_re"][0], p["lam_im"][0], p["log_dt"][0], p["b_re"][0], p["b_im"][0],
                       p["c_re"][0], p["c_im"][0], reverse=False)
    y_b, h_b = s5_scan(ug, h0_b, p["lam_re"][1], p["lam_im"][1], p["log_dt"][1], p["b_re"][1], p["b_im"][1],
                       p["c_re"][1], p["c_im"][1], reverse=True)
    y = (y_f + y_b).reshape(bn, seq_len, D_SSM) + p["d_skip"].astype(jnp.float32) * u.astype(jnp.float32)
    g = jax.nn.gelu(y)
    out = g * jax.nn.sigmoid(g @ p["w_glu"].astype(jnp.float32) + p["b_glu"].astype(jnp.float32))
    return out.astype(u.dtype), h_f, h_b


def short_conv(v, w, b, rows):
    bn, seq_len, ch = v.shape
    width = seq_len // rows
    vg = v.reshape(bn, rows, width, ch)
    vp = jnp.pad(vg, ((0, 0), (0, 0), (CONV_WIDTH // 2, CONV_WIDTH // 2), (0, 0)))
    y = sum(w[k] * vp[:, :, k:k + width] for k in range(CONV_WIDTH)) + b
    return y.reshape(bn, seq_len, ch)


def token_mixer(h, h0_f, h0_b, rows, p):
    bn, seq_len, _ = h.shape
    z = h @ p["w_in"]
    u, cb, cc, ch = jnp.split(z, [D_SSM, D_SSM + D_CONV, D_SSM + 2 * D_CONV], axis=-1)
    y_ssm, h_f, h_b = ssm_mixer(u, h0_f, h0_b, p)
    y_conv = cb * short_conv(cc * ch, p["conv_w"], p["conv_b"], rows)
    g = p["out_norm_g"]
    y_ssm = rmsnorm(y_ssm, g[:D_SSM])
    hd = D_CONV // CONV_HEADS
    y_conv = rmsnorm(y_conv.reshape(bn, seq_len, CONV_HEADS, hd), g[D_SSM:].reshape(CONV_HEADS, hd))
    y = jnp.concatenate([y_ssm, y_conv.reshape(bn, seq_len, D_CONV)], axis=-1)
    return y @ p["w_out"], h_f, h_b


def peer(h, w_query, sub_keys, expert_u, expert_v):
    bn, seq_len, d = h.shape
    xs = h.reshape(-1, TOKEN_BLOCK, d)
    sk = sub_keys.astype(jnp.float32)

    def block(xb):
        q = (xb @ w_query).astype(jnp.float32).reshape(TOKEN_BLOCK, PEER_HEADS, 2, PEER_DQ // 2)
        s = jnp.einsum("thse,sne->thsn", q, sk)
        s1, i1 = lax.top_k(s[:, :, 0], PEER_TOPK)
        s2, i2 = lax.top_k(s[:, :, 1], PEER_TOPK)
        cand = (s1[..., :, None] + s2[..., None, :]).reshape(TOKEN_BLOCK, PEER_HEADS, PEER_TOPK * PEER_TOPK)
        cidx = (i1[..., :, None] * PEER_NKEYS + i2[..., None, :]).reshape(TOKEN_BLOCK, PEER_HEADS, PEER_TOPK * PEER_TOPK)
        top, pos = lax.top_k(cand, PEER_TOPK)
        idx = jnp.take_along_axis(cidx, pos, axis=-1)
        gate = jax.nn.softmax(top, axis=-1)
        hid = jnp.einsum("thkd,td->thk", expert_u[idx], xb).astype(jnp.float32)
        act = (jax.nn.gelu(hid) * gate).astype(xb.dtype)
        return jnp.einsum("thk,thkd->td", act, expert_v[idx])

    return lax.map(block, xs).reshape(bn, seq_len, d)


def modulation(cond, w_mod, b_mod):
    m = jax.nn.silu(cond.astype(jnp.float32)) @ w_mod.astype(jnp.float32) + b_mod.astype(jnp.float32)
    return jnp.split(m, N_MOD, axis=-1)


def layer(x, cond, h0_f, h0_b, rows, p):
    sh1, sc1, g1, sh2, sc2, g2 = [m.astype(x.dtype) for m in modulation(cond, p["w_mod"], p["b_mod"])]
    h = rmsnorm(x, p["norm1_g"]) * (1 + sc1) + sh1
    y, h_f, h_b = token_mixer(h, h0_f, h0_b, rows, p)
    x = x + g1 * y
    h = rmsnorm(x, p["norm2_g"]) * (1 + sc2) + sh2
    x = x + g2 * peer(h, p["w_query"], p["sub_keys"], p["expert_u"], p["expert_v"])
    return x, h_f, h_b


def setup_inputs(seed: int = 0) -> dict:
    key = jax.random.key(seed)
    ks = jax.random.split(key, 32)
    f32 = jnp.float32

    def nrm(k, shape, scale):
        return scale * jax.random.normal(k, shape, f32)

    g_shape = (DEPTH, 2, N_SSM_GROUPS, SSM_STATE)
    n_idx = jnp.arange(SSM_STATE, dtype=f32)
    return {
        "x_prompt": nrm(ks[0], (BATCH, SEQ, D_MODEL), 1.0),
        "x_sample": nrm(ks[1], (DEC_BATCH, DEC_SEQ, D_MODEL), 1.0),
        "state_ssm_re": nrm(ks[2], (DEC_BATCH, DEPTH, 2, N_SSM_GROUPS, SSM_STATE), 0.1),
        "state_ssm_im": nrm(ks[3], (DEC_BATCH, DEPTH, 2, N_SSM_GROUPS, SSM_STATE), 0.1),
        "c": nrm(ks[4], (DEC_BATCH, D_MODEL), 1.0),
        "c_ctx": nrm(ks[5], (D_MODEL,), 1.0),
        "w_mod": nrm(ks[6], (DEPTH, D_MODEL, N_MOD * D_MODEL), 0.5 * D_MODEL ** -0.5),
        "b_mod": nrm(ks[7], (DEPTH, N_MOD * D_MODEL), 0.02),
        "norm1_g": 1.0 + nrm(ks[8], (DEPTH, D_MODEL), 0.05),
        "norm2_g": 1.0 + nrm(ks[9], (DEPTH, D_MODEL), 0.05),
        "w_in": nrm(ks[10], (DEPTH, D_MODEL, D_IN), D_MODEL ** -0.5),
        "lam_re": -0.5 + nrm(ks[11], g_shape, 0.01),
        "lam_im": math.pi * n_idx + nrm(ks[12], g_shape, 0.01),
        "log_dt": jax.random.uniform(ks[13], (DEPTH, 2, N_SSM_GROUPS), f32, math.log(1e-3), math.log(1e-1)),
        "b_re": nrm(ks[14], (DEPTH, 2, N_SSM_GROUPS, SSM_STATE, SSM_CH), (2 * SSM_CH) ** -0.5),
        "b_im": nrm(ks[15], (DEPTH, 2, N_SSM_GROUPS, SSM_STATE, SSM_CH), (2 * SSM_CH) ** -0.5),
        "c_re": nrm(ks[16], (DEPTH, 2, N_SSM_GROUPS, SSM_CH, SSM_STATE), (2 * SSM_STATE) ** -0.5),
        "c_im": nrm(ks[17], (DEPTH, 2, N_SSM_GROUPS, SSM_CH, SSM_STATE), (2 * SSM_STATE) ** -0.5),
        "d_skip": nrm(ks[18], (DEPTH, D_SSM), 1.0),
        "w_glu": nrm(ks[19], (DEPTH, D_SSM, D_SSM), D_SSM ** -0.5),
        "b_glu": nrm(ks[20], (DEPTH, D_SSM), 0.02),
        "conv_w": nrm(ks[21], (DEPTH, CONV_WIDTH, D_CONV), 0.5),
        "conv_b": nrm(ks[22], (DEPTH, D_CONV), 0.02),
        "out_norm_g": 1.0 + nrm(ks[23], (DEPTH, D_MIX), 0.05),
        "w_out": nrm(ks[24], (DEPTH, D_MIX, D_MODEL), D_MIX ** -0.5),
        "w_query": nrm(ks[25], (DEPTH, D_MODEL, PEER_HEADS * PEER_DQ), D_MODEL ** -0.5),
        "sub_keys": nrm(ks[26], (DEPTH, 2, PEER_NKEYS, PEER_DQ // 2), (PEER_DQ // 2) ** -0.5),
        "expert_u": nrm(ks[27], (DEPTH, N_EXPERTS, D_MODEL), D_MODEL ** -0.5),
        "expert_v": nrm(ks[28], (DEPTH, N_EXPERTS, D_MODEL), 0.5),
        "final_norm_g": 1.0 + nrm(ks[29], (D_MODEL,), 0.05),
    }


def reference(x_prompt, x_sample, state_ssm_re, state_ssm_im, c, c_ctx, w_mod, b_mod, norm1_g, norm2_g, w_in,
              lam_re, lam_im, log_dt, b_re, b_im, c_re, c_im, d_skip, w_glu, b_glu, conv_w, conv_b,
              out_norm_g, w_out, w_query, sub_keys, expert_u, expert_v, final_norm_g):
    f32 = jnp.float32
    rows = x_sample.shape[1] // GRID_W
    xp, xs = x_prompt, x_sample
    zero_state = jnp.zeros((x_prompt.shape[0], N_SSM_GROUPS, SSM_STATE), jnp.complex64)
    ctx_states = []
    for l in range(DEPTH):
        p = {
            "w_mod": w_mod[l], "b_mod": b_mod[l], "norm1_g": norm1_g[l], "norm2_g": norm2_g[l],
            "w_in": w_in[l], "lam_re": lam_re[l], "lam_im": lam_im[l], "log_dt": log_dt[l],
            "b_re": b_re[l], "b_im": b_im[l], "c_re": c_re[l], "c_im": c_im[l], "d_skip": d_skip[l],
            "w_glu": w_glu[l], "b_glu": b_glu[l], "conv_w": conv_w[l], "conv_b": conv_b[l],
            "out_norm_g": out_norm_g[l], "w_out": w_out[l], "w_query": w_query[l], "sub_keys": sub_keys[l],
            "expert_u": expert_u[l], "expert_v": expert_v[l],
        }
        xp, h_f, h_b = layer(xp, c_ctx[None, None, :], zero_state, zero_state, 1, p)
        ctx_states.append(jnp.stack([h_f, h_b], axis=1))
        h0_f = lax.complex(state_ssm_re[:, l, 0].astype(f32), state_ssm_im[:, l, 0].astype(f32))
        h0_b = lax.complex(state_ssm_re[:, l, 1].astype(f32), state_ssm_im[:, l, 1].astype(f32))
        xs, _, _ = layer(xs, c[:, None, :], h0_f, h0_b, rows, p)
    new_state = jnp.stack(ctx_states, axis=1)
    y_prompt = rmsnorm(xp, final_norm_g)
    y_sample = rmsnorm(xs, final_norm_g)
    return (y_prompt, y_sample, new_state.real, new_state.imag)
```

```python
import functools
import math

import jax
import jax.numpy as jnp
from jax import lax
from jax.experimental import pallas as pl
from jax.experimental.pallas import tpu as pltpu

F32 = jnp.float32
BF16 = jnp.bfloat16

D_MODEL = 1024
N_TOK = 8192
N_PATH_TOK = 4096
N_ROWS = 16
N_TAU = 256
D_SSM = 512
SSM_CH = 16
SSM_STATE = 64
N_GROUPS = 32
D_CONV = 512
CONV_HEADS = 8
N_MOD = 6
PEER_HEADS = 8
NKEYS = 128
TOPK = 16
N_EXPERTS = NKEYS * NKEYS
EPS = 1e-6
GRID_W = 64

SSM_CHUNKS = 4
CHUNK_U = D_SSM // SSM_CHUNKS
CHUNK_N = CHUNK_U // SSM_CH * SSM_STATE
SUB_N = 256

TM_ROW = 256
TM_PEER = 512
TE_PEER = 1024
assert TE_PEER // NKEYS == 8
VMEM_LIMIT = 56 * 1024 * 1024

NEG_INF = float("-inf")
POS_INF = float("inf")


def _split_bf16(x):
    hi = x.astype(BF16)
    lo = (x - hi.astype(F32)).astype(BF16)
    return hi, lo


def _dot(a, b):
    return jnp.dot(a, b, preferred_element_type=F32)


def _dot_nt(a, b):
    return lax.dot_general(a, b, (((1,), (1,)), ((), ())), preferred_element_type=F32)


def _rms(x):
    return x * lax.rsqrt(jnp.mean(x * x, axis=-1, keepdims=True) + EPS)


def _mod_kernel(c_ref, w_ref, b_ref, o_ref):
    c = c_ref[...]
    s = c * jax.nn.sigmoid(c)
    s_hi, s_lo = _split_bf16(s)
    w_hi, w_lo = _split_bf16(w_ref[...])
    o_ref[...] = _dot(s_hi, w_hi) + _dot(s_lo, w_hi) + _dot(s_hi, w_lo) + b_ref[...]


def _modulation(cond8, w_mod, b_mod):
    n = w_mod.shape[1]
    bn = 1024
    return pl.pallas_call(
        _mod_kernel,
        grid=(n // bn,),
        in_specs=[
            pl.BlockSpec((8, D_MODEL), lambda j: (0, 0)),
            pl.BlockSpec((D_MODEL, bn), lambda j: (0, j)),
            pl.BlockSpec((1, bn), lambda j: (0, j)),
        ],
        out_specs=pl.BlockSpec((8, bn), lambda j: (0, j)),
        out_shape=jax.ShapeDtypeStruct((8, n), F32),
        compiler_params=pltpu.CompilerParams(
            dimension_semantics=("arbitrary",), vmem_limit_bytes=VMEM_LIMIT),
    )(cond8, w_mod, b_mod)


def _inproj_kernel(x_ref, sc_ref, sh_ref, g_ref, w_ref, z_ref):
    h = _rms(x_ref[...]) * g_ref[...]
    h = h * (1.0 + sc_ref[0]) + sh_ref[0]
    z_ref[...] = _dot(h.astype(BF16), w_ref[...])


def _inproj(x, sc1, sh1, g, w_in):
    tiles_per_path = N_PATH_TOK // TM_ROW
    d_in = w_in.shape[1]
    pat = pl.BlockSpec((1, TM_ROW, D_MODEL), lambda i: (i // tiles_per_path, 0, 0))
    return pl.pallas_call(
        _inproj_kernel,
        grid=(N_TOK // TM_ROW,),
        in_specs=[
            pl.BlockSpec((TM_ROW, D_MODEL), lambda i: (i, 0)),
            pat, pat,
            pl.BlockSpec((1, D_MODEL), lambda i: (0, 0)),
            pl.BlockSpec((D_MODEL, d_in), lambda i: (0, 0)),
        ],
        out_specs=pl.BlockSpec((TM_ROW, d_in), lambda i: (i, 0)),
        out_shape=jax.ShapeDtypeStruct((N_TOK, d_in), F32),
        compiler_params=pltpu.CompilerParams(
            dimension_semantics=("arbitrary",), vmem_limit_bytes=VMEM_LIMIT),
    )(x, sc1, sh1, g, w_in)


def _cmul(ar, ai, br, bi):
    return ar * br - ai * bi, ar * bi + ai * br


def _shift_rows(x, forward):
    shift = 1 if forward else 7
    return jnp.concatenate(
        [pltpu.roll(x[0:8], shift, 0), pltpu.roll(x[8:16], shift, 0)], axis=0)


def _ssm_conv_kernel(u_ref, cb_ref, cc_ref, ch_ref, wb_ref, cm_ref, a_ref, h0_ref,
                     cw_ref, cbias_ref, ys_ref, yc_ref, st_ref, bu_ref, vpad_ref):
    path = pl.program_id(0)
    row_blk = 512
    n_blk = N_PATH_TOK // row_blk

    for d in range(2):
        reverse = d == 1

        def bu_body(rb, _):
            r0 = pl.multiple_of(rb * row_blk, row_blk)
            ub = u_ref[pl.ds(r0, row_blk), :].astype(BF16)
            bu_ref[pl.ds(r0, row_blk), :] = _dot(ub, wb_ref[d, 0])
            return 0
        lax.fori_loop(0, n_blk, bu_body, 0)

        for s in range(CHUNK_N // SUB_N):
            re_cols = pl.ds(s * SUB_N, SUB_N)
            im_cols = pl.ds(CHUNK_N + s * SUB_N, SUB_N)
            ar = jnp.broadcast_to(a_ref[d, 0, 0:1, s * SUB_N:(s + 1) * SUB_N], (N_ROWS, SUB_N))
            ai = jnp.broadcast_to(a_ref[d, 0, 1:2, s * SUB_N:(s + 1) * SUB_N], (N_ROWS, SUB_N))

            def row_of(k):
                tau = (N_TAU - 1 - k) if reverse else k
                return pl.multiple_of(tau * N_ROWS, N_ROWS)

            def scan_body(k, carry):
                hr, hi = carry
                rows = pl.ds(row_of(k), N_ROWS)
                pr, pi = _cmul(ar, ai, hr, hi)
                hr = pr + bu_ref[rows, re_cols]
                hi = pi + bu_ref[rows, im_cols]
                bu_ref[rows, re_cols] = hr
                bu_ref[rows, im_cols] = hi
                return hr, hi

            zero = jnp.zeros((N_ROWS, SUB_N), F32)
            hr_end, hi_end = lax.fori_loop(0, N_TAU, scan_body, (zero, zero), unroll=4)
            st_ref[0, d, 0, :, s * SUB_N:(s + 1) * SUB_N] = hr_end
            st_ref[0, d, 1, :, s * SUB_N:(s + 1) * SUB_N] = hi_end

            @pl.when(path == 1)
            def _():
                a256r, a256i = ar, ai
                for _ in range(8):
                    a256r, a256i = _cmul(a256r, a256i, a256r, a256i)
                h0r = h0_ref[0, d, 0, :, s * SUB_N:(s + 1) * SUB_N]
                h0i = h0_ref[0, d, 1, :, s * SUB_N:(s + 1) * SUB_N]
                seg = lax.broadcasted_iota(jnp.int32, (N_ROWS, SUB_N), 0) % 8
                first = seg == (7 if reverse else 0)
                hin_r, hin_i = h0r, h0i
                for _ in range(7):
                    tr, ti = _cmul(a256r, a256i, hin_r, hin_i)
                    tr = _shift_rows(tr + hr_end, not reverse)
                    ti = _shift_rows(ti + hi_end, not reverse)
                    hin_r = jnp.where(first, h0r, tr)
                    hin_i = jnp.where(first, h0i, ti)

                def fix_body(k, carry):
                    qr, qi = carry
                    rows = pl.ds(row_of(k), N_ROWS)
                    qr, qi = _cmul(ar, ai, qr, qi)
                    bu_ref[rows, re_cols] = bu_ref[rows, re_cols] + qr
                    bu_ref[rows, im_cols] = bu_ref[rows, im_cols] + qi
                    return qr, qi
                lax.fori_loop(0, N_TAU, fix_body, (hin_r, hin_i), unroll=4)

        def y_body(rb, _):
            r0 = pl.multiple_of(rb * row_blk, row_blk)
            hb = bu_ref[pl.ds(r0, row_blk), :].astype(BF16)
            y = _dot(hb, cm_ref[d, 0])
            if d == 0:
                ys_ref[pl.ds(r0, row_blk), :] = y
            else:
                ys_ref[pl.ds(r0, row_blk), :] = ys_ref[pl.ds(r0, row_blk), :] + y
            return 0
        lax.fori_loop(0, n_blk, y_body, 0)

    width = jnp.where(path == 0, N_TAU, GRID_W)
    pad = jnp.zeros((N_ROWS, CHUNK_U), F32)
    vpad_ref[0:N_ROWS, :] = pad
    vpad_ref[N_PATH_TOK + N_ROWS:N_PATH_TOK + 2 * N_ROWS, :] = pad

    def v_body(rb, _):
        r0 = pl.multiple_of(rb * row_blk, row_blk)
        vpad_ref[pl.ds(r0 + N_ROWS, row_blk), :] = (
            cc_ref[pl.ds(r0, row_blk), :] * ch_ref[pl.ds(r0, row_blk), :])
        return 0
    lax.fori_loop(0, n_blk, v_body, 0)

    w0 = cw_ref[0:1, :]
    w1 = cw_ref[1:2, :]
    w2 = cw_ref[2:3, :]
    bias = cbias_ref[...]

    def conv_body(rb, _):
        r0 = pl.multiple_of(rb * row_blk, row_blk)
        tau = (r0 + lax.broadcasted_iota(jnp.int32, (row_blk, CHUNK_U), 0)) // N_ROWS
        col = tau & (width - 1)
        prev = jnp.where(col != 0, vpad_ref[pl.ds(r0, row_blk), :], 0.0)
        cur = vpad_ref[pl.ds(r0 + N_ROWS, row_blk), :]
        nxt = jnp.where(col != width - 1, vpad_ref[pl.ds(r0 + 2 * N_ROWS, row_blk), :], 0.0)
        y = w0 * prev + w1 * cur + w2 * nxt + bias
        yc_ref[pl.ds(r0, row_blk), :] = cb_ref[pl.ds(r0, row_blk), :] * y
        return 0
    lax.fori_loop(0, n_blk, conv_body, 0)


def _ssm_conv(z, wb, cm, a, h0, conv_w, conv_b):
    nc = SSM_CHUNKS
    zcol = lambda off: pl.BlockSpec((N_PATH_TOK, CHUNK_U), lambda p, c: (p, off * nc + c))
    out_blk = pl.BlockSpec((N_PATH_TOK, CHUNK_U), lambda p, c: (p, c))
    st_blk = pl.BlockSpec((1, 2, 2, N_ROWS, CHUNK_N), lambda p, c: (p, 0, 0, 0, c))
    return pl.pallas_call(
        _ssm_conv_kernel,
        grid=(2, nc),
        in_specs=[
            zcol(0), zcol(1), zcol(2), zcol(3),
            pl.BlockSpec((2, 1, CHUNK_U, 2 * CHUNK_N), lambda p, c: (0, c, 0, 0)),
            pl.BlockSpec((2, 1, 2 * CHUNK_N, CHUNK_U), lambda p, c: (0, c, 0, 0)),
            pl.BlockSpec((2, 1, 2, CHUNK_N), lambda p, c: (0, c, 0, 0)),
            st_blk,
            pl.BlockSpec((3, CHUNK_U), lambda p, c: (0, c)),
            pl.BlockSpec((1, CHUNK_U), lambda p, c: (0, c)),
        ],
        out_specs=[out_blk, out_blk, st_blk],
        out_shape=[
            jax.ShapeDtypeStruct((N_TOK, D_SSM), F32),
            jax.ShapeDtypeStruct((N_TOK, D_CONV), F32),
            jax.ShapeDtypeStruct((2, 2, 2, N_ROWS, N_GROUPS * SSM_STATE), F32),
        ],
        scratch_shapes=[
            pltpu.VMEM((N_PATH_TOK, 2 * CHUNK_N), F32),
            pltpu.VMEM((N_PATH_TOK + 2 * N_ROWS, CHUNK_U), F32),
        ],
        compiler_params=pltpu.CompilerParams(
            dimension_semantics=("arbitrary", "arbitrary"), vmem_limit_bytes=VMEM_LIMIT),
    )(z, z, z, z, wb, cm, a, h0, conv_w, conv_b)


def _mixout_kernel(x_ref, u_ref, ys_ref, yc_ref, g1_ref, sc2_ref, sh2_ref, dsk_ref,
                   wglu_ref, bglu_ref, ong_ref, seg_ref, wout_ref, n2g_ref,
                   x1_ref, h2_ref):
    y = ys_ref[...] + dsk_ref[...] * u_ref[...]
    g = jax.nn.gelu(y)
    gate = jax.nn.sigmoid(_dot(g.astype(BF16), wglu_ref[...]) + bglu_ref[...])
    o = _rms(g * gate) * ong_ref[:, 0:D_SSM]
    yc = yc_ref[...]
    sq_hi, sq_lo = _split_bf16(yc * yc)
    head_ms = _dot(sq_hi, seg_ref[...]) + _dot(sq_lo, seg_ref[...])
    ycn = yc * lax.rsqrt(head_ms + EPS) * ong_ref[:, D_SSM:D_SSM + D_CONV]
    ymix = (_dot(o.astype(BF16), wout_ref[0:D_SSM, :])
            + _dot(ycn.astype(BF16), wout_ref[D_SSM:D_SSM + D_CONV, :]))
    x1 = x_ref[...] + g1_ref[0] * ymix
    x1_ref[...] = x1
    h2 = _rms(x1) * n2g_ref[...]
    h2_ref[...] = (h2 * (1.0 + sc2_ref[0]) + sh2_ref[0]).astype(BF16)


def _mixout(x, z, ys, yc, g1, sc2, sh2, d_skip, w_glu, b_glu, ong, seg, w_out, n2g):
    tiles_per_path = N_PATH_TOK // TM_ROW
    pat = pl.BlockSpec((1, TM_ROW, D_MODEL), lambda i: (i // tiles_per_path, 0, 0))
    row = lambda w: pl.BlockSpec((TM_ROW, w), lambda i: (i, 0))
    full = lambda a: pl.BlockSpec(a.shape, lambda i: (0,) * a.ndim)
    return pl.pallas_call(
        _mixout_kernel,
        grid=(N_TOK // TM_ROW,),
        in_specs=[row(D_MODEL), row(D_SSM), row(D_SSM), row(D_CONV), pat, pat, pat,
                  full(d_skip), full(w_glu), full(b_glu), full(ong), full(seg),
                  full(w_out), full(n2g)],
        out_specs=[row(D_MODEL), row(D_MODEL)],
        out_shape=[jax.ShapeDtypeStruct((N_TOK, D_MODEL), F32),
                   jax.ShapeDtypeStruct((N_TOK, D_MODEL), BF16)],
        compiler_params=pltpu.CompilerParams(
            dimension_semantics=("arbitrary",), vmem_limit_bytes=VMEM_LIMIT),
    )(x, z, ys, yc, g1, sc2, sh2, d_skip, w_glu, b_glu, ong, seg, w_out, n2g)


def _oddeven_merge_sort_pairs(n):
    pairs = []

    def merge(lo, hi, r):
        step = r * 2
        if step < hi - lo:
            merge(lo, hi, step)
            merge(lo + r, hi, step)
            for i in range(lo + r, hi - r, step):
                pairs.append((i, i + r))
        else:
            pairs.append((lo, lo + r))

    def sort(lo, hi):
        if hi - lo >= 1:
            mid = lo + (hi - lo) // 2
            sort(lo, mid)
            sort(mid + 1, hi)
            merge(lo, hi, 1)

    sort(0, n - 1)
    return pairs


_SORT16 = _oddeven_merge_sort_pairs(16)


def _sort16_desc(v):
    v = list(v)
    for i, j in _SORT16:
        hi = jnp.maximum(v[i], v[j])
        lo = jnp.minimum(v[i], v[j])
        v[i], v[j] = hi, lo
    return v


def _bitonic_merge_desc(v):
    v = list(v)
    d = 8
    while d >= 1:
        for i in range(16):
            if (i & d) == 0:
                hi = jnp.maximum(v[i], v[i + d])
                lo = jnp.minimum(v[i], v[i + d])
                v[i], v[i + d] = hi, lo
        d //= 2
    return v


def _top16_of_two(x, y):
    return _bitonic_merge_desc([jnp.maximum(x[r], y[15 - r]) for r in range(16)])


def _top16_sublanes(s):
    v = _sort16_desc([s[8 * r:8 * r + 8, :] for r in range(16)])
    for shift in (4, 2, 1):
        w = [pltpu.roll(x, shift, 0) for x in v]
        v = _top16_of_two(v, w)
    return v


_CAND = [(a, b) for a in range(16) for b in range(16) if (a + 1) * (b + 1) <= TOPK]


def _peer_select(s0, s1):
    a_top = _top16_sublanes(s0)
    b_top = _top16_sublanes(s1)
    cand = {(a, b): a_top[a] + b_top[b] for a, b in _CAND}
    neg = jnp.full_like(a_top[0], NEG_INF)
    rest = [cand[p] for p in _CAND if p[0] != 0]
    l0 = [cand[(0, b)] for b in range(16)]
    l1 = _sort16_desc(rest[0:16])
    l2 = _sort16_desc(rest[16:32])
    l3 = _sort16_desc(rest[32:] + [neg] * (48 - len(rest)))
    m1 = _top16_of_two(l0, l1)
    m2 = _top16_of_two(l2, l3)
    thr = functools.reduce(jnp.minimum, [jnp.maximum(m1[r], m2[15 - r]) for r in range(16)])
    top = cand[(0, 0)]
    zsum = jnp.zeros_like(top)
    tau = [jnp.full_like(top, POS_INF) for _ in range(16)]
    for a, b in _CAND:
        sel = cand[(a, b)] >= thr
        zsum = zsum + jnp.where(sel, jnp.exp(cand[(a, b)] - top), 0.0)
        tau[a] = jnp.where(sel, b_top[b], tau[a])
    inv_z = 1.0 / zsum
    tau_rows, e0_rows = [], []
    for r in range(16):
        s0r = s0[8 * r:8 * r + 8, :]
        t = jnp.full_like(s0r, POS_INF)
        for a in range(15, -1, -1):
            t = jnp.where(s0r >= a_top[a], tau[a], t)
        tau_rows.append(t)
        e0_rows.append(jnp.exp(s0r - a_top[0]) * inv_z)
    e1 = jnp.exp(s1 - jnp.concatenate([b_top[0]] * 16, axis=0))
    return jnp.concatenate(tau_rows, axis=0), jnp.concatenate(e0_rows, axis=0), e1


def _peer_kernel(h2_ref, x1_ref, g2_ref, fg_ref, wq_ref, sk_ref, u_ref, vt_ref, y_ref,
                 s0_ref, s1_ref, tau_ref, e0_ref, e1_ref, hid_ref, act_ref, acc_ref):
    e = pl.program_id(1)
    n_e = pl.num_programs(1)
    n_lg = TM_PEER // 128
    i_per_step = TE_PEER // NKEYS

    @pl.when(e == 0)
    def _():
        h2 = h2_ref[...]
        for h in range(PEER_HEADS):
            qt = _dot_nt(wq_ref[h * 256:(h + 1) * 256, :], h2)
            s0_ref[h] = _dot(sk_ref[0], qt[0:128].astype(BF16))
            s1_ref[h] = _dot(sk_ref[1], qt[128:256].astype(BF16))

        def sel_body(idx, _):
            h = idx // n_lg
            lanes = pl.ds(pl.multiple_of((idx % n_lg) * 128, 128), 128)
            tau, e0, e1 = _peer_select(s0_ref[h, :, lanes], s1_ref[h, :, lanes])
            tau_ref[h, :, lanes] = tau
            e0_ref[h, :, lanes] = e0
            e1_ref[h, :, lanes] = e1
            return 0
        lax.fori_loop(0, PEER_HEADS * n_lg, sel_body, 0)
        acc_ref[...] = jnp.zeros_like(acc_ref)

    hid_ref[...] = _dot_nt(u_ref[...], h2_ref[...])

    i_rows = pl.ds(pl.multiple_of(e * i_per_step, i_per_step), i_per_step)

    def gate_body(lg, _):
        lanes = pl.ds(pl.multiple_of(lg * 128, 128), 128)
        tau8 = [tau_ref[h, i_rows, lanes] for h in range(PEER_HEADS)]
        e08 = [e0_ref[h, i_rows, lanes] for h in range(PEER_HEADS)]
        for ii in range(i_per_step):
            rows = pl.ds(ii * NKEYS, NKEYS)
            gsum = jnp.zeros((NKEYS, 128), F32)
            for h in range(PEER_HEADS):
                gsum = gsum + jnp.where(s1_ref[h, :, lanes] >= tau8[h][ii:ii + 1],
                                        e1_ref[h, :, lanes] * e08[h][ii:ii + 1], 0.0)
            act = jax.nn.gelu(hid_ref[rows, lanes]) * gsum
            act_ref[rows, lanes] = act.astype(BF16)
        return 0
    lax.fori_loop(0, n_lg, gate_body, 0)

    acc_ref[...] += _dot(vt_ref[...], act_ref[...])

    @pl.when(e == n_e - 1)
    def _():
        x2 = x1_ref[...] + g2_ref[0] * acc_ref[...].T
        y_ref[...] = _rms(x2) * fg_ref[...]


def _peer(h2, x1, g2, fg, wq_t, sk, u, vt):
    nt = N_TOK // TM_PEER
    ne = N_EXPERTS // TE_PEER
    tiles_per_path = N_PATH_TOK // TM_PEER
    tok = lambda w: pl.BlockSpec((TM_PEER, w), lambda t, e: (t, 0))
    sel = pltpu.VMEM((PEER_HEADS, NKEYS, TM_PEER), F32)
    return pl.pallas_call(
        _peer_kernel,
        grid=(nt, ne),
        in_specs=[
            tok(D_MODEL), tok(D_MODEL),
            pl.BlockSpec((1, TM_PEER, D_MODEL), lambda t, e: (t // tiles_per_path, 0, 0)),
            pl.BlockSpec((1, D_MODEL), lambda t, e: (0, 0)),
            pl.BlockSpec(wq_t.shape, lambda t, e: (0, 0)),
            pl.BlockSpec(sk.shape, lambda t, e: (0, 0, 0)),
            pl.BlockSpec((TE_PEER, D_MODEL), lambda t, e: (e, 0)),
            pl.BlockSpec((D_MODEL, TE_PEER), lambda t, e: (0, e)),
        ],
        out_specs=tok(D_MODEL),
        out_shape=jax.ShapeDtypeStruct((N_TOK, D_MODEL), F32),
        scratch_shapes=[
            sel, sel, sel, sel, sel,
            pltpu.VMEM((TE_PEER, TM_PEER), F32),
            pltpu.VMEM((TE_PEER, TM_PEER), BF16),
            pltpu.VMEM((D_MODEL, TM_PEER), F32),
        ],
        compiler_params=pltpu.CompilerParams(
            dimension_semantics=("arbitrary", "arbitrary"), vmem_limit_bytes=VMEM_LIMIT),
    )(h2, x1, g2, fg, wq_t, sk, u, vt)


def _block_diag_chunks(w, rows_per_group, cols_per_group):
    gpc = N_GROUPS // SSM_CHUNKS
    w = w.reshape(SSM_CHUNKS, gpc, rows_per_group, cols_per_group)
    eye = jnp.eye(gpc, dtype=w.dtype)
    out = jnp.einsum("cgrk,gh->cgrhk", w, eye)
    return out.reshape(SSM_CHUNKS, gpc * rows_per_group, gpc * cols_per_group)


def _ssm_params(lam_re, lam_im, log_dt, b_re, b_im, c_re, c_im):
    lam = lax.complex(lam_re, lam_im)
    dt = jnp.exp(log_dt)[..., None]
    a_bar = jnp.exp(lam * dt)
    b_bar = ((a_bar - 1.0) / lam)[..., None] * lax.complex(b_re, b_im)
    wb, cm = [], []
    for d in range(2):
        bt = jnp.swapaxes(b_bar[d], 1, 2)
        wb.append(jnp.concatenate(
            [_block_diag_chunks(bt.real, SSM_CH, SSM_STATE),
             _block_diag_chunks(bt.imag, SSM_CH, SSM_STATE)], axis=-1))
        ct_re = jnp.swapaxes(c_re[d], 1, 2)
        ct_im = jnp.swapaxes(c_im[d], 1, 2)
        cm.append(jnp.concatenate(
            [_block_diag_chunks(ct_re, SSM_STATE, SSM_CH),
             _block_diag_chunks(-ct_im, SSM_STATE, SSM_CH)], axis=-2))
    wb = jnp.stack(wb).astype(BF16)
    cm = jnp.stack(cm).astype(BF16)
    a = jnp.stack([a_bar.real.reshape(2, SSM_CHUNKS, CHUNK_N),
                   a_bar.imag.reshape(2, SSM_CHUNKS, CHUNK_N)], axis=2)
    return wb, cm, a


def kernel(x_prompt, x_sample, state_ssm_re, state_ssm_im, c, c_ctx, w_mod, b_mod, norm1_g, norm2_g, w_in, lam_re, lam_im, log_dt, b_re, b_im, c_re, c_im, d_skip, w_glu, b_glu, conv_w, conv_b, out_norm_g, w_out, w_query, sub_keys, expert_u, expert_v, final_norm_g):
    n_b, n_seq, _ = x_prompt.shape
    n_db, n_dseq, _ = x_sample.shape
    n_seg = n_dseq // N_TAU
    lyr = 0

    xp = x_prompt.transpose(1, 0, 2).reshape(N_PATH_TOK, D_MODEL)
    xs = x_sample.reshape(n_db, n_seg, N_TAU, D_MODEL).transpose(2, 0, 1, 3).reshape(N_PATH_TOK, D_MODEL)
    x = jnp.concatenate([xp, xs], axis=0)

    cond8 = jnp.zeros((8, D_MODEL), F32).at[0].set(c_ctx).at[1:1 + n_db].set(c)
    mod = _modulation(cond8, w_mod[lyr], b_mod[lyr].reshape(1, -1))
    row_src = jnp.array([0] * N_ROWS + [1 + r // n_seg for r in range(N_ROWS)], jnp.int32)

    def pattern(k, tile):
        m = mod[row_src, k * D_MODEL:(k + 1) * D_MODEL].reshape(2, 1, N_ROWS, D_MODEL)
        return jnp.broadcast_to(m, (2, tile // N_ROWS, N_ROWS, D_MODEL)).reshape(2, tile, D_MODEL)

    sh1, sc1, g1, sh2, sc2 = (pattern(k, TM_ROW) for k in range(5))
    g2 = pattern(5, TM_PEER)

    z = _inproj(x, sc1, sh1, norm1_g[lyr].reshape(1, -1), w_in[lyr].astype(BF16))

    wb, cm, a = _ssm_params(lam_re[lyr], lam_im[lyr], log_dt[lyr], b_re[lyr], b_im[lyr],
                            c_re[lyr], c_im[lyr])
    st = jnp.stack([state_ssm_re[:, lyr], state_ssm_im[:, lyr]], axis=2)
    st = st.reshape(n_db, 2, 2, N_GROUPS * SSM_STATE).transpose(1, 2, 0, 3)
    h0 = jnp.zeros((2, 2, 2, n_db, n_seg, N_GROUPS * SSM_STATE), F32)
    h0 = h0.at[1, 0, :, :, 0].set(st[0]).at[1, 1, :, :, n_seg - 1].set(st[1])
    h0 = h0.reshape(2, 2, 2, N_ROWS, N_GROUPS * SSM_STATE)

    ys, yc, st_out = _ssm_conv(z, wb, cm, a, h0, conv_w[lyr], conv_b[lyr].reshape(1, -1))

    seg = jnp.kron(jnp.eye(CONV_HEADS, dtype=F32),
                   jnp.full((D_CONV // CONV_HEADS,) * 2, CONV_HEADS / D_CONV, F32)).astype(BF16)
    x1, h2 = _mixout(x, z, ys, yc, g1, sc2, sh2, d_skip[lyr].reshape(1, -1),
                     w_glu[lyr].astype(BF16), b_glu[lyr].reshape(1, -1),
                     out_norm_g[lyr].reshape(1, -1), seg, w_out[lyr].astype(BF16),
                     norm2_g[lyr].reshape(1, -1))

    y = _peer(h2, x1, g2, final_norm_g.reshape(1, -1),
              w_query[lyr].T.astype(BF16), sub_keys[lyr].astype(BF16),
              expert_u[lyr].astype(BF16), expert_v[lyr].T.astype(BF16))

    y_prompt = y[:N_PATH_TOK].reshape(n_seq, n_b, D_MODEL).transpose(1, 0, 2)
    y_sample = y[N_PATH_TOK:].reshape(N_TAU, n_db, n_seg, D_MODEL).transpose(1, 2, 0, 3)
    y_sample = y_sample.reshape(n_db, n_dseq, D_MODEL)
    new_state = st_out[0].reshape(2, 2, n_b, N_GROUPS, SSM_STATE).transpose(1, 2, 0, 3, 4)
    new_re = new_state[0][:, None]
    new_im = new_state[1][:, None]
    return y_prompt, y_sample, new_re, new_im
```

```python
import functools
import math

import jax
import jax.numpy as jnp
from jax import lax
from jax.experimental import pallas as pl
from jax.experimental.pallas import tpu as pltpu

F32 = jnp.float32
BF16 = jnp.bfloat16

D_MODEL = 1024
N_TOK = 8192
N_PATH_TOK = 4096
N_ROWS = 16
N_TAU = 256
D_SSM = 512
SSM_CH = 16
SSM_STATE = 64
N_GROUPS = 32
D_CONV = 512
CONV_HEADS = 8
N_MOD = 6
PEER_HEADS = 8
NKEYS = 128
TOPK = 16
N_EXPERTS = NKEYS * NKEYS
EPS = 1e-6
GRID_W = 64

SSM_CHUNKS = 4
CHUNK_U = D_SSM // SSM_CHUNKS
CHUNK_N = CHUNK_U // SSM_CH * SSM_STATE
SUB_N = 256

TM_ROW = 256
TM_PEER = 512
TE_PEER = 1024
assert TE_PEER // NKEYS == 8
VMEM_LIMIT = 56 * 1024 * 1024

NEG_INF = float("-inf")
POS_INF = float("inf")


def _split_bf16(x):
    hi = x.astype(BF16)
    lo = (x - hi.astype(F32)).astype(BF16)
    return hi, lo


def _dot(a, b):
    return jnp.dot(a, b, preferred_element_type=F32)


def _dot_nt(a, b):
    return lax.dot_general(a, b, (((1,), (1,)), ((), ())), preferred_element_type=F32)


def _rms(x):
    return x * lax.rsqrt(jnp.mean(x * x, axis=-1, keepdims=True) + EPS)


def _mod_kernel(c_ref, w_ref, b_ref, o_ref):
    c = c_ref[...]
    s = c * jax.nn.sigmoid(c)
    s_hi, s_lo = _split_bf16(s)
    w_hi, w_lo = _split_bf16(w_ref[...])
    o_ref[...] = _dot(s_hi, w_hi) + _dot(s_lo, w_hi) + _dot(s_hi, w_lo) + b_ref[...]


def _modulation(cond8, w_mod, b_mod):
    n = w_mod.shape[1]
    bn = 1024
    return pl.pallas_call(
        _mod_kernel,
        grid=(n // bn,),
        in_specs=[
            pl.BlockSpec((8, D_MODEL), lambda j: (0, 0)),
            pl.BlockSpec((D_MODEL, bn), lambda j: (0, j)),
            pl.BlockSpec((1, bn), lambda j: (0, j)),
        ],
        out_specs=pl.BlockSpec((8, bn), lambda j: (0, j)),
        out_shape=jax.ShapeDtypeStruct((8, n), F32),
        compiler_params=pltpu.CompilerParams(
            dimension_semantics=("arbitrary",), vmem_limit_bytes=VMEM_LIMIT),
    )(cond8, w_mod, b_mod)


def _inproj_kernel(x_ref, sc_ref, sh_ref, g_ref, w_ref, z_ref):
    h = _rms(x_ref[...]) * g_ref[...]
    h = h * (1.0 + sc_ref[0]) + sh_ref[0]
    z_ref[...] = _dot(h.astype(BF16), w_ref[...])


def _inproj(x, sc1, sh1, g, w_in):
    tiles_per_path = N_PATH_TOK // TM_ROW
    d_in = w_in.shape[1]
    pat = pl.BlockSpec((1, TM_ROW, D_MODEL), lambda i: (i // tiles_per_path, 0, 0))
    return pl.pallas_call(
        _inproj_kernel,
        grid=(N_TOK // TM_ROW,),
        in_specs=[
            pl.BlockSpec((TM_ROW, D_MODEL), lambda i: (i, 0)),
            pat, pat,
            pl.BlockSpec((1, D_MODEL), lambda i: (0, 0)),
            pl.BlockSpec((D_MODEL, d_in), lambda i: (0, 0)),
        ],
        out_specs=pl.BlockSpec((TM_ROW, d_in), lambda i: (i, 0)),
        out_shape=jax.ShapeDtypeStruct((N_TOK, d_in), F32),
        compiler_params=pltpu.CompilerParams(
            dimension_semantics=("arbitrary",), vmem_limit_bytes=VMEM_LIMIT),
    )(x, sc1, sh1, g, w_in)


def _cmul(ar, ai, br, bi):
    return ar * br - ai * bi, ar * bi + ai * br


def _shift_rows(x, forward):
    shift = 1 if forward else 7
    return jnp.concatenate(
        [pltpu.roll(x[0:8], shift, 0), pltpu.roll(x[8:16], shift, 0)], axis=0)


def _ssm_conv_kernel(u_ref, cb_ref, cc_ref, ch_ref, wb_ref, cm_ref, a_ref, h0_ref,
                     cw_ref, cbias_ref, ys_ref, yc_ref, st_ref, bu_ref, vpad_ref):
    path = pl.program_id(0)
    row_blk = 512
    n_blk = N_PATH_TOK // row_blk

    for d in range(2):
        reverse = d == 1

        def bu_body(rb, _):
            r0 = pl.multiple_of(rb * row_blk, row_blk)
            ub = u_ref[pl.ds(r0, row_blk), :].astype(BF16)
            bu_ref[pl.ds(r0, row_blk), :] = _dot(ub, wb_ref[d, 0])
            return 0
        lax.fori_loop(0, n_blk, bu_body, 0)

        for s in range(CHUNK_N // SUB_N):
            re_cols = pl.ds(s * SUB_N, SUB_N)
            im_cols = pl.ds(CHUNK_N + s * SUB_N, SUB_N)
            ar = jnp.broadcast_to(a_ref[d, 0, 0:1, s * SUB_N:(s + 1) * SUB_N], (N_ROWS, SUB_N))
            ai = jnp.broadcast_to(a_ref[d, 0, 1:2, s * SUB_N:(s + 1) * SUB_N], (N_ROWS, SUB_N))

            def row_of(k):
                tau = (N_TAU - 1 - k) if reverse else k
                return pl.multiple_of(tau * N_ROWS, N_ROWS)

            def scan_body(k, carry):
                hr, hi = carry
                rows = pl.ds(row_of(k), N_ROWS)
                pr, pi = _cmul(ar, ai, hr, hi)
                hr = pr + bu_ref[rows, re_cols]
                hi = pi + bu_ref[rows, im_cols]
                bu_ref[rows, re_cols] = hr
                bu_ref[rows, im_cols] = hi
                return hr, hi

            zero = jnp.zeros((N_ROWS, SUB_N), F32)
            hr_end, hi_end = lax.fori_loop(0, N_TAU, scan_body, (zero, zero), unroll=4)
            st_ref[0, d, 0, :, s * SUB_N:(s + 1) * SUB_N] = hr_end
            st_ref[0, d, 1, :, s * SUB_N:(s + 1) * SUB_N] = hi_end

            @pl.when(path == 1)
            def _():
                a256r, a256i = ar, ai
                for _ in range(8):
                    a256r, a256i = _cmul(a256r, a256i, a256r, a256i)
                h0r = h0_ref[0, d, 0, :, s * SUB_N:(s + 1) * SUB_N]
                h0i = h0_ref[0, d, 1, :, s * SUB_N:(s + 1) * SUB_N]
                seg = lax.broadcasted_iota(jnp.int32, (N_ROWS, SUB_N), 0) % 8
                first = seg == (7 if reverse else 0)
                hin_r, hin_i = h0r, h0i
                for _ in range(7):
                    tr, ti = _cmul(a256r, a256i, hin_r, hin_i)
                    tr = _shift_rows(tr + hr_end, not reverse)
                    ti = _shift_rows(ti + hi_end, not reverse)
                    hin_r = jnp.where(first, h0r, tr)
                    hin_i = jnp.where(first, h0i, ti)

                def fix_body(k, carry):
                    qr, qi = carry
                    rows = pl.ds(row_of(k), N_ROWS)
                    qr, qi = _cmul(ar, ai, qr, qi)
                    bu_ref[rows, re_cols] = bu_ref[rows, re_cols] + qr
                    bu_ref[rows, im_cols] = bu_ref[rows, im_cols] + qi
                    return qr, qi
                lax.fori_loop(0, N_TAU, fix_body, (hin_r, hin_i), unroll=4)

        def y_body(rb, _):
            r0 = pl.multiple_of(rb * row_blk, row_blk)
            hb = bu_ref[pl.ds(r0, row_blk), :].astype(BF16)
            y = _dot(hb, cm_ref[d, 0])
            if d == 0:
                ys_ref[pl.ds(r0, row_blk), :] = y
            else:
                ys_ref[pl.ds(r0, row_blk), :] = ys_ref[pl.ds(r0, row_blk), :] + y
            return 0
        lax.fori_loop(0, n_blk, y_body, 0)

    width = jnp.where(path == 0, N_TAU, GRID_W)
    pad = jnp.zeros((N_ROWS, CHUNK_U), F32)
    vpad_ref[0:N_ROWS, :] = pad
    vpad_ref[N_PATH_TOK + N_ROWS:N_PATH_TOK + 2 * N_ROWS, :] = pad

    def v_body(rb, _):
        r0 = pl.multiple_of(rb * row_blk, row_blk)
        vpad_ref[pl.ds(r0 + N_ROWS, row_blk), :] = (
            cc_ref[pl.ds(r0, row_blk), :] * ch_ref[pl.ds(r0, row_blk), :])
        return 0
    lax.fori_loop(0, n_blk, v_body, 0)

    w0 = cw_ref[0:1, :]
    w1 = cw_ref[1:2, :]
    w2 = cw_ref[2:3, :]
    bias = cbias_ref[...]

    def conv_body(rb, _):
        r0 = pl.multiple_of(rb * row_blk, row_blk)
        tau = (r0 + lax.broadcasted_iota(jnp.int32, (row_blk, CHUNK_U), 0)) // N_ROWS
        col = tau & (width - 1)
        prev = jnp.where(col != 0, vpad_ref[pl.ds(r0, row_blk), :], 0.0)
        cur = vpad_ref[pl.ds(r0 + N_ROWS, row_blk), :]
        nxt = jnp.where(col != width - 1, vpad_ref[pl.ds(r0 + 2 * N_ROWS, row_blk), :], 0.0)
        y = w0 * prev + w1 * cur + w2 * nxt + bias
        yc_ref[pl.ds(r0, row_blk), :] = cb_ref[pl.ds(r0, row_blk), :] * y
        return 0
    lax.fori_loop(0, n_blk, conv_body, 0)


def _ssm_conv(z, wb, cm, a, h0, conv_w, conv_b):
    nc = SSM_CHUNKS
    zcol = lambda off: pl.BlockSpec((N_PATH_TOK, CHUNK_U), lambda p, c: (p, off * nc + c))
    out_blk = pl.BlockSpec((N_PATH_TOK, CHUNK_U), lambda p, c: (p, c))
    st_blk = pl.BlockSpec((1, 2, 2, N_ROWS, CHUNK_N), lambda p, c: (p, 0, 0, 0, c))
    return pl.pallas_call(
        _ssm_conv_kernel,
        grid=(2, nc),
        in_specs=[
            zcol(0), zcol(1), zcol(2), zcol(3),
            pl.BlockSpec((2, 1, CHUNK_U, 2 * CHUNK_N), lambda p, c: (0, c, 0, 0)),
            pl.BlockSpec((2, 1, 2 * CHUNK_N, CHUNK_U), lambda p, c: (0, c, 0, 0)),
            pl.BlockSpec((2, 1, 2, CHUNK_N), lambda p, c: (0, c, 0, 0)),
            st_blk,
            pl.BlockSpec((3, CHUNK_U), lambda p, c: (0, c)),
            pl.BlockSpec((1, CHUNK_U), lambda p, c: (0, c)),
        ],
        out_specs=[out_blk, out_blk, st_blk],
        out_shape=[
            jax.ShapeDtypeStruct((N_TOK, D_SSM), F32),
            jax.ShapeDtypeStruct((N_TOK, D_CONV), F32),
            jax.ShapeDtypeStruct((2, 2, 2, N_ROWS, N_GROUPS * SSM_STATE), F32),
        ],
        scratch_shapes=[
            pltpu.VMEM((N_PATH_TOK, 2 * CHUNK_N), F32),
            pltpu.VMEM((N_PATH_TOK + 2 * N_ROWS, CHUNK_U), F32),
        ],
        compiler_params=pltpu.CompilerParams(
            dimension_semantics=("arbitrary", "arbitrary"), vmem_limit_bytes=VMEM_LIMIT),
    )(z, z, z, z, wb, cm, a, h0, conv_w, conv_b)


def _mixout_kernel(x_ref, u_ref, ys_ref, yc_ref, g1_ref, sc2_ref, sh2_ref, dsk_ref,
                   wglu_ref, bglu_ref, ong_ref, seg_ref, wout_ref, n2g_ref,
                   x1_ref, h2t_ref):
    y = ys_ref[...] + dsk_ref[...] * u_ref[...]
    g = jax.nn.gelu(y)
    gate = jax.nn.sigmoid(_dot(g.astype(BF16), wglu_ref[...]) + bglu_ref[...])
    o = _rms(g * gate) * ong_ref[:, 0:D_SSM]
    yc = yc_ref[...]
    sq_hi, sq_lo = _split_bf16(yc * yc)
    head_ms = _dot(sq_hi, seg_ref[...]) + _dot(sq_lo, seg_ref[...])
    ycn = yc * lax.rsqrt(head_ms + EPS) * ong_ref[:, D_SSM:D_SSM + D_CONV]
    ymix = (_dot(o.astype(BF16), wout_ref[0:D_SSM, :])
            + _dot(ycn.astype(BF16), wout_ref[D_SSM:D_SSM + D_CONV, :]))
    x1 = x_ref[...] + g1_ref[0] * ymix
    x1_ref[...] = x1
    h2 = _rms(x1) * n2g_ref[...]
    h2t_ref[...] = (h2 * (1.0 + sc2_ref[0]) + sh2_ref[0]).T.astype(BF16)


def _mixout(x, z, ys, yc, g1, sc2, sh2, d_skip, w_glu, b_glu, ong, seg, w_out, n2g):
    tiles_per_path = N_PATH_TOK // TM_ROW
    pat = pl.BlockSpec((1, TM_ROW, D_MODEL), lambda i: (i // tiles_per_path, 0, 0))
    row = lambda w: pl.BlockSpec((TM_ROW, w), lambda i: (i, 0))
    full = lambda a: pl.BlockSpec(a.shape, lambda i: (0,) * a.ndim)
    return pl.pallas_call(
        _mixout_kernel,
        grid=(N_TOK // TM_ROW,),
        in_specs=[row(D_MODEL), row(D_SSM), row(D_SSM), row(D_CONV), pat, pat, pat,
                  full(d_skip), full(w_glu), full(b_glu), full(ong), full(seg),
                  full(w_out), full(n2g)],
        out_specs=[row(D_MODEL), pl.BlockSpec((D_MODEL, TM_ROW), lambda i: (0, i))],
        out_shape=[jax.ShapeDtypeStruct((N_TOK, D_MODEL), F32),
                   jax.ShapeDtypeStruct((D_MODEL, N_TOK), BF16)],
        compiler_params=pltpu.CompilerParams(
            dimension_semantics=("arbitrary",), vmem_limit_bytes=VMEM_LIMIT),
    )(x, z, ys, yc, g1, sc2, sh2, d_skip, w_glu, b_glu, ong, seg, w_out, n2g)


def _oddeven_merge_sort_pairs(n):
    pairs = []

    def merge(lo, hi, r):
        step = r * 2
        if step < hi - lo:
            merge(lo, hi, step)
            merge(lo + r, hi, step)
            for i in range(lo + r, hi - r, step):
                pairs.append((i, i + r))
        else:
            pairs.append((lo, lo + r))

    def sort(lo, hi):
        if hi - lo >= 1:
            mid = lo + (hi - lo) // 2
            sort(lo, mid)
            sort(mid + 1, hi)
            merge(lo, hi, 1)

    sort(0, n - 1)
    return pairs


_SORT16 = _oddeven_merge_sort_pairs(16)


def _sort16_desc(v):
    v = list(v)
    for i, j in _SORT16:
        hi = jnp.maximum(v[i], v[j])
        lo = jnp.minimum(v[i], v[j])
        v[i], v[j] = hi, lo
    return v


def _bitonic_merge_desc(v):
    v = list(v)
    d = 8
    while d >= 1:
        for i in range(16):
            if (i & d) == 0:
                hi = jnp.maximum(v[i], v[i + d])
                lo = jnp.minimum(v[i], v[i + d])
                v[i], v[i + d] = hi, lo
        d //= 2
    return v


def _top16_of_two(x, y):
    return _bitonic_merge_desc([jnp.maximum(x[r], y[15 - r]) for r in range(16)])


def _top16_sublanes(s):
    v = _sort16_desc([s[8 * r:8 * r + 8, :] for r in range(16)])
    for shift in (4, 2, 1):
        w = [pltpu.roll(x, shift, 0) for x in v]
        v = _top16_of_two(v, w)
    return v


_CAND = [(a, b) for a in range(16) for b in range(16) if (a + 1) * (b + 1) <= TOPK]


def _peer_select(s0, s1):
    a_top = _top16_sublanes(s0)
    b_top = _top16_sublanes(s1)
    cand = {(a, b): a_top[a] + b_top[b] for a, b in _CAND}
    neg = jnp.full_like(a_top[0], NEG_INF)
    rest = [cand[p] for p in _CAND if p[0] != 0]
    l0 = [cand[(0, b)] for b in range(16)]
    l1 = _sort16_desc(rest[0:16])
    l2 = _sort16_desc(rest[16:32])
    l3 = _sort16_desc(rest[32:] + [neg] * (48 - len(rest)))
    m1 = _top16_of_two(l0, l1)
    m2 = _top16_of_two(l2, l3)
    thr = functools.reduce(jnp.minimum, [jnp.maximum(m1[r], m2[15 - r]) for r in range(16)])
    top = cand[(0, 0)]
    zsum = jnp.zeros_like(top)
    tau = [jnp.full_like(top, POS_INF) for _ in range(16)]
    for a, b in _CAND:
        sel = cand[(a, b)] >= thr
        zsum = zsum + jnp.where(sel, jnp.exp(cand[(a, b)] - top), 0.0)
        tau[a] = jnp.where(sel, b_top[b], tau[a])
    inv_z = 1.0 / zsum
    tau_rows, e0_rows = [], []
    for r in range(16):
        s0r = s0[8 * r:8 * r + 8, :]
        t = jnp.full_like(s0r, POS_INF)
        for a in range(15, -1, -1):
            t = jnp.where(s0r >= a_top[a], tau[a], t)
        tau_rows.append(t)
        e0_rows.append(jnp.exp(s0r - a_top[0]) * inv_z)
    e1 = jnp.exp(s1 - jnp.concatenate([b_top[0]] * 16, axis=0))
    return jnp.concatenate(tau_rows, axis=0), jnp.concatenate(e0_rows, axis=0), e1


def _peer_kernel(h2t_ref, x1_ref, g2_ref, fg_ref, wq_ref, sk_ref, u_ref, vt_ref, y_ref,
                 s0_ref, s1_ref, tau_ref, e0_ref, e1_ref, hid_ref, act_ref, acc_ref):
    e = pl.program_id(1)
    n_e = pl.num_programs(1)
    n_lg = TM_PEER // 128
    i_per_step = TE_PEER // NKEYS

    @pl.when(e == 0)
    def _():
        h2t = h2t_ref[...]
        for h in range(PEER_HEADS):
            qt = _dot(wq_ref[h * 256:(h + 1) * 256, :], h2t)
            s0 = _dot(sk_ref[0], qt[0:128].astype(BF16))
            s1 = _dot(sk_ref[1], qt[128:256].astype(BF16))
            for lg in range(n_lg):
                s0_ref[h * n_lg + lg] = s0[:, lg * 128:(lg + 1) * 128]
                s1_ref[h * n_lg + lg] = s1[:, lg * 128:(lg + 1) * 128]

        def sel_body(idx, _):
            tau, e0, e1 = _peer_select(s0_ref[idx], s1_ref[idx])
            tau_ref[idx] = tau.reshape(NKEYS // 8, 8, 128)
            e0_ref[idx] = e0.reshape(NKEYS // 8, 8, 128)
            e1_ref[idx] = e1
            return 0
        lax.fori_loop(0, PEER_HEADS * n_lg, sel_body, 0)
        acc_ref[...] = jnp.zeros_like(acc_ref)

    hid = _dot(u_ref[...], h2t_ref[...])
    for lg in range(n_lg):
        hid_ref[lg] = hid[:, lg * 128:(lg + 1) * 128]

    def gate_body(lg, _):
        lanes = pl.ds(pl.multiple_of(lg * 128, 128), 128)
        for ii in range(i_per_step):
            rows = pl.ds(ii * NKEYS, NKEYS)
            gsum = jnp.zeros((NKEYS, 128), F32)
            for h in range(PEER_HEADS):
                slab = h * n_lg + lg
                tau_i = tau_ref[slab, e, pl.ds(ii, 1), :]
                e0_i = e0_ref[slab, e, pl.ds(ii, 1), :]
                gsum = gsum + jnp.where(s1_ref[slab] >= tau_i, e1_ref[slab] * e0_i, 0.0)
            act = jax.nn.gelu(hid_ref[lg, rows, :]) * gsum
            act_ref[rows, lanes] = act.astype(BF16)
        return 0
    lax.fori_loop(0, n_lg, gate_body, 0)

    acc_ref[...] += _dot(vt_ref[...], act_ref[...])

    @pl.when(e == n_e - 1)
    def _():
        x2 = x1_ref[...] + g2_ref[0] * acc_ref[...].T
        y_ref[...] = _rms(x2) * fg_ref[...]


def _peer(h2t, x1, g2, fg, wq_t, sk, u, vt):
    nt = N_TOK // TM_PEER
    ne = N_EXPERTS // TE_PEER
    tiles_per_path = N_PATH_TOK // TM_PEER
    tok = lambda w: pl.BlockSpec((TM_PEER, w), lambda t, e: (t, 0))
    n_slab = PEER_HEADS * (TM_PEER // 128)
    sel = pltpu.VMEM((n_slab, NKEYS, 128), F32)
    sel_i = pltpu.VMEM((n_slab, NKEYS // 8, 8, 128), F32)
    return pl.pallas_call(
        _peer_kernel,
        grid=(nt, ne),
        in_specs=[
            pl.BlockSpec((D_MODEL, TM_PEER), lambda t, e: (0, t)),
            tok(D_MODEL),
            pl.BlockSpec((1, TM_PEER, D_MODEL), lambda t, e: (t // tiles_per_path, 0, 0)),
            pl.BlockSpec((1, D_MODEL), lambda t, e: (0, 0)),
            pl.BlockSpec(wq_t.shape, lambda t, e: (0, 0)),
            pl.BlockSpec(sk.shape, lambda t, e: (0, 0, 0)),
            pl.BlockSpec((TE_PEER, D_MODEL), lambda t, e: (e, 0)),
            pl.BlockSpec((D_MODEL, TE_PEER), lambda t, e: (0, e)),
        ],
        out_specs=tok(D_MODEL),
        out_shape=jax.ShapeDtypeStruct((N_TOK, D_MODEL), F32),
        scratch_shapes=[
            sel, sel, sel_i, sel_i, sel,
            pltpu.VMEM((TM_PEER // 128, TE_PEER, 128), F32),
            pltpu.VMEM((TE_PEER, TM_PEER), BF16),
            pltpu.VMEM((D_MODEL, TM_PEER), F32),
        ],
        compiler_params=pltpu.CompilerParams(
            dimension_semantics=("arbitrary", "arbitrary"), vmem_limit_bytes=VMEM_LIMIT),
    )(h2t, x1, g2, fg, wq_t, sk, u, vt)


def _block_diag_chunks(w, rows_per_group, cols_per_group):
    gpc = N_GROUPS // SSM_CHUNKS
    w = w.reshape(SSM_CHUNKS, gpc, rows_per_group, cols_per_group)
    eye = jnp.eye(gpc, dtype=w.dtype)
    out = jnp.einsum("cgrk,gh->cgrhk", w, eye)
    return out.reshape(SSM_CHUNKS, gpc * rows_per_group, gpc * cols_per_group)


def _ssm_params(lam_re, lam_im, log_dt, b_re, b_im, c_re, c_im):
    lam = lax.complex(lam_re, lam_im)
    dt = jnp.exp(log_dt)[..., None]
    a_bar = jnp.exp(lam * dt)
    b_bar = ((a_bar - 1.0) / lam)[..., None] * lax.complex(b_re, b_im)
    wb, cm = [], []
    for d in range(2):
        bt = jnp.swapaxes(b_bar[d], 1, 2)
        wb.append(jnp.concatenate(
            [_block_diag_chunks(bt.real, SSM_CH, SSM_STATE),
             _block_diag_chunks(bt.imag, SSM_CH, SSM_STATE)], axis=-1))
        ct_re = jnp.swapaxes(c_re[d], 1, 2)
        ct_im = jnp.swapaxes(c_im[d], 1, 2)
        cm.append(jnp.concatenate(
            [_block_diag_chunks(ct_re, SSM_STATE, SSM_CH),
             _block_diag_chunks(-ct_im, SSM_STATE, SSM_CH)], axis=-2))
    wb = jnp.stack(wb).astype(BF16)
    cm = jnp.stack(cm).astype(BF16)
    a = jnp.stack([a_bar.real.reshape(2, SSM_CHUNKS, CHUNK_N),
                   a_bar.imag.reshape(2, SSM_CHUNKS, CHUNK_N)], axis=2)
    return wb, cm, a


def kernel(x_prompt, x_sample, state_ssm_re, state_ssm_im, c, c_ctx, w_mod, b_mod, norm1_g, norm2_g, w_in, lam_re, lam_im, log_dt, b_re, b_im, c_re, c_im, d_skip, w_glu, b_glu, conv_w, conv_b, out_norm_g, w_out, w_query, sub_keys, expert_u, expert_v, final_norm_g):
    n_b, n_seq, _ = x_prompt.shape
    n_db, n_dseq, _ = x_sample.shape
    n_seg = n_dseq // N_TAU
    lyr = 0

    xp = x_prompt.transpose(1, 0, 2).reshape(N_PATH_TOK, D_MODEL)
    xs = x_sample.reshape(n_db, n_seg, N_TAU, D_MODEL).transpose(2, 0, 1, 3).reshape(N_PATH_TOK, D_MODEL)
    x = jnp.concatenate([xp, xs], axis=0)

    cond8 = jnp.zeros((8, D_MODEL), F32).at[0].set(c_ctx).at[1:1 + n_db].set(c)
    mod = _modulation(cond8, w_mod[lyr], b_mod[lyr].reshape(1, -1))
    def pattern(k, tile):
        m = mod[:, k * D_MODEL:(k + 1) * D_MODEL]
        ctx = jnp.broadcast_to(m[0:1], (N_ROWS, D_MODEL))
        lat = jnp.broadcast_to(m[1:1 + n_db, None], (n_db, n_seg, D_MODEL)).reshape(N_ROWS, D_MODEL)
        m = jnp.stack([ctx, lat]).reshape(2, 1, N_ROWS, D_MODEL)
        return jnp.broadcast_to(m, (2, tile // N_ROWS, N_ROWS, D_MODEL)).reshape(2, tile, D_MODEL)

    sh1, sc1, g1, sh2, sc2 = (pattern(k, TM_ROW) for k in range(5))
    g2 = pattern(5, TM_PEER)

    z = _inproj(x, sc1, sh1, norm1_g[lyr].reshape(1, -1), w_in[lyr].astype(BF16))

    wb, cm, a = _ssm_params(lam_re[lyr], lam_im[lyr], log_dt[lyr], b_re[lyr], b_im[lyr],
                            c_re[lyr], c_im[lyr])
    st = jnp.stack([state_ssm_re[:, lyr], state_ssm_im[:, lyr]], axis=2)
    st = st.reshape(n_db, 2, 2, N_GROUPS * SSM_STATE).transpose(1, 2, 0, 3)
    h0 = jnp.zeros((2, 2, 2, n_db, n_seg, N_GROUPS * SSM_STATE), F32)
    h0 = h0.at[1, 0, :, :, 0].set(st[0]).at[1, 1, :, :, n_seg - 1].set(st[1])
    h0 = h0.reshape(2, 2, 2, N_ROWS, N_GROUPS * SSM_STATE)

    ys, yc, st_out = _ssm_conv(z, wb, cm, a, h0, conv_w[lyr], conv_b[lyr].reshape(1, -1))

    seg = jnp.kron(jnp.eye(CONV_HEADS, dtype=F32),
                   jnp.full((D_CONV // CONV_HEADS,) * 2, CONV_HEADS / D_CONV, F32)).astype(BF16)
    x1, h2t = _mixout(x, z, ys, yc, g1, sc2, sh2, d_skip[lyr].reshape(1, -1),
                     w_glu[lyr].astype(BF16), b_glu[lyr].reshape(1, -1),
                     out_norm_g[lyr].reshape(1, -1), seg, w_out[lyr].astype(BF16),
                     norm2_g[lyr].reshape(1, -1))

    y = _peer(h2t, x1, g2, final_norm_g.reshape(1, -1),
              w_query[lyr].T.astype(BF16), sub_keys[lyr].astype(BF16),
              expert_u[lyr].astype(BF16), expert_v[lyr].T.astype(BF16))

    y_prompt = y[:N_PATH_TOK].reshape(n_seq, n_b, D_MODEL).transpose(1, 0, 2)
    y_sample = y[N_PATH_TOK:].reshape(N_TAU, n_db, n_seg, D_MODEL).transpose(1, 2, 0, 3)
    y_sample = y_sample.reshape(n_db, n_dseq, D_MODEL)
    new_state = st_out[0].reshape(2, 2, n_b, N_GROUPS, SSM_STATE).transpose(1, 2, 0, 3, 4)
    new_re = new_state[0][:, None]
    new_im = new_state[1][:, None]
    return y_prompt, y_sample, new_re, new_im
```
